```python
import jax, jax.numpy as jnp
from jax import lax
import numpy as np

D_MODEL = 1024
BATCH = 1
SEQ = 16384
DEPTH = 4

MIX_WIDTH = D_MODEL
POOL_WIDTH = D_MODEL // 4
SCONV_WIDTH = 3 * D_MODEL // 8
CCONV_WIDTH = MIX_WIDTH - POOL_WIDTH - SCONV_WIDTH
HEAD_DIM = 64
POOL_WINDOWS = (2, 4, 8, 16)
N_POOL_GROUPS = len(POOL_WINDOWS)
POOL_GROUP = POOL_WIDTH // N_POOL_GROUPS
SCONV_K = 3
CCONV_K = 31
IN_COLS = POOL_WIDTH + 3 * SCONV_WIDTH + 2 * CCONV_WIDTH
D_FF = 2816
LN_EPS = 1e-5
DEEPNORM_ALPHA = (2.0 * DEPTH) ** 0.25
DEEPNORM_BETA = (8.0 * DEPTH) ** -0.25

kernel_name = "hybrid_pool_conv_conformer_encoder"


def layer_norm(x, g, b):
    xf = x.astype(jnp.float32)
    mu = jnp.mean(xf, axis=-1, keepdims=True)
    var = jnp.mean(jnp.square(xf - mu), axis=-1, keepdims=True)
    y = (xf - mu) * lax.rsqrt(var + LN_EPS)
    return (y * g.astype(jnp.float32) + b.astype(jnp.float32)).astype(x.dtype)


def swiglu_ffn(x, w_gate, w_up, w_down):
    return (jax.nn.silu(x @ w_gate) * (x @ w_up)) @ w_down


def depthwise_conv(u, w):
    k = w.shape[0]
    return lax.conv_general_dilated(
        u, w[:, None, :], window_strides=(1,), padding=[(k // 2, k // 2)],
        dimension_numbers=("NWC", "WIO", "NWC"), feature_group_count=u.shape[-1])


def centred_pool_minus_self(u, window):
    seq = u.shape[1]
    uf = u.astype(jnp.float32)
    csum = jnp.pad(jnp.cumsum(uf, axis=1), ((0, 0), (1, 0), (0, 0)))
    t = jnp.arange(seq)
    left = window // 2
    lo = jnp.clip(t - left, 0, seq)
    hi = jnp.clip(t - left + window, 0, seq)
    total = jnp.take(csum, hi, axis=1) - jnp.take(csum, lo, axis=1)
    count = (hi - lo).astype(jnp.float32)
    return (total / count[None, :, None] - uf).astype(u.dtype)


def hybrid_mixer(h, w_in, pool_w, pool_scale, sconv_w, cconv_w, cconv_b, cnorm_g, cnorm_b, w_out):
    bsz, seq = h.shape[0], h.shape[1]
    proj = h @ w_in
    cuts = [POOL_WIDTH,
            POOL_WIDTH + SCONV_WIDTH,
            POOL_WIDTH + 2 * SCONV_WIDTH,
            POOL_WIDTH + 3 * SCONV_WIDTH,
            POOL_WIDTH + 3 * SCONV_WIDTH + CCONV_WIDTH]
    u_pool, gate_b, gate_c, v, c_val, c_gate = jnp.split(proj, cuts, axis=-1)

    pooled = jnp.stack(
        [centred_pool_minus_self(u_pool[..., g * POOL_GROUP:(g + 1) * POOL_GROUP], w)
         for g, w in enumerate(POOL_WINDOWS)], axis=2)
    y_a = jnp.einsum("bsgc,gcd->bsgd", pooled, pool_w).reshape(bsz, seq, POOL_WIDTH) * pool_scale

    y_b = gate_b * depthwise_conv(gate_c * v, sconv_w)

    a = c_val * jax.nn.sigmoid(c_gate)
    a = depthwise_conv(a, cconv_w) + cconv_b
    y_c = jax.nn.silu(layer_norm(a, cnorm_g, cnorm_b))

    return jnp.concatenate([y_a, y_b, y_c], axis=-1) @ w_out


def setup_inputs(seed: int = 0) -> dict:
    key = jax.random.key(seed)
    ks = jax.random.split(key, 22)

    def nrm(k, shape, scale):
        return jax.random.normal(k, shape, jnp.float32) * scale

    d, f = D_MODEL, D_FF
    return {
        "x": nrm(ks[0], (BATCH, SEQ, d), 1.0),
        "ln1_g": 1.0 + nrm(ks[1], (DEPTH, d), 0.05),
        "ln1_b": nrm(ks[2], (DEPTH, d), 0.02),
        "ffn1_w_gate": nrm(ks[3], (DEPTH, d, f), d ** -0.5),
        "ffn1_w_up": nrm(ks[4], (DEPTH, d, f), d ** -0.5),
        "ffn1_w_down": nrm(ks[5], (DEPTH, f, d), DEEPNORM_BETA * f ** -0.5),
        "mix_w_in": nrm(ks[6], (DEPTH, d, IN_COLS), d ** -0.5),
        "pool_w": nrm(ks[7], (DEPTH, N_POOL_GROUPS, POOL_GROUP, POOL_GROUP), POOL_GROUP ** -0.5),
        "pool_scale": 1.0 + nrm(ks[8], (DEPTH, POOL_WIDTH), 0.05),
        "sconv_w": nrm(ks[9], (DEPTH, SCONV_K, SCONV_WIDTH), SCONV_K ** -0.5),
        "cconv_w": nrm(ks[10], (DEPTH, CCONV_K, CCONV_WIDTH), CCONV_K ** -0.5),
        "cconv_b": nrm(ks[11], (DEPTH, CCONV_WIDTH), 0.02),
        "cnorm_g": 1.0 + nrm(ks[12], (DEPTH, CCONV_WIDTH), 0.05),
        "cnorm_b": nrm(ks[13], (DEPTH, CCONV_WIDTH), 0.02),
        "mix_w_out": nrm(ks[14], (DEPTH, MIX_WIDTH, d), DEEPNORM_BETA * MIX_WIDTH ** -0.5),
        "ln2_g": 1.0 + nrm(ks[15], (DEPTH, d), 0.05),
        "ln2_b": nrm(ks[16], (DEPTH, d), 0.02),
        "ffn2_w_gate": nrm(ks[17], (DEPTH, d, f), d ** -0.5),
        "ffn2_w_up": nrm(ks[18], (DEPTH, d, f), d ** -0.5),
        "ffn2_w_down": nrm(ks[19], (DEPTH, f, d), DEEPNORM_BETA * f ** -0.5),
        "ln3_g": 1.0 + nrm(ks[20], (DEPTH, d), 0.05),
        "ln3_b": nrm(ks[21], (DEPTH, d), 0.02),
    }


def reference(x, ln1_g, ln1_b, ffn1_w_gate, ffn1_w_up, ffn1_w_down, mix_w_in, pool_w,
              pool_scale, sconv_w, cconv_w, cconv_b, cnorm_g, cnorm_b, mix_w_out,
              ln2_g, ln2_b, ffn2_w_gate, ffn2_w_up, ffn2_w_down, ln3_g, ln3_b):
    for l in range(DEPTH):
        x = layer_norm(DEEPNORM_ALPHA * x
                       + 0.5 * swiglu_ffn(x, ffn1_w_gate[l], ffn1_w_up[l], ffn1_w_down[l]),
                       ln1_g[l], ln1_b[l])
        x = layer_norm(DEEPNORM_ALPHA * x
                       + hybrid_mixer(x, mix_w_in[l], pool_w[l], pool_scale[l], sconv_w[l],
                                      cconv_w[l], cconv_b[l], cnorm_g[l], cnorm_b[l], mix_w_out[l]),
                       ln2_g[l], ln2_b[l])
        x = layer_norm(DEEPNORM_ALPHA * x
                       + 0.5 * swiglu_ffn(x, ffn2_w_gate[l], ffn2_w_up[l], ffn2_w_down[l]),
                       ln3_g[l], ln3_b[l])
    return x
```

```python
import functools

import jax
import jax.numpy as jnp
from jax import lax
from jax.experimental import pallas as pl
from jax.experimental.pallas import tpu as pltpu

LN_EPS = 1e-5
POOL_WINDOWS = (2, 4, 8, 16)
SUBLANES = 8
HALO = 16
FFN_TILE_M = 512
MIX_TILE_M = 512
MIX_CONV_ROWS = 32
VMEM_LIMIT_BYTES = 56 * 1024 * 1024

_F32 = jnp.float32
_BF16 = jnp.bfloat16


def _layer_norm(z, g, b):
    mu = jnp.mean(z, axis=-1, keepdims=True)
    zc = z - mu
    var = jnp.mean(zc * zc, axis=-1, keepdims=True)
    return zc * lax.rsqrt(var + LN_EPS) * g + b


def _ffn_ln_kernel(x_ref, wg_ref, wu_ref, wd_ref, g_ref, b_ref, o_ref, h_ref, *, alpha, f_chunks):
    x = x_ref[...]
    xb = x.astype(_BF16)
    off = 0
    for fc in f_chunks:
        gate = jnp.dot(xb, wg_ref[:, off:off + fc], preferred_element_type=_F32)
        up = jnp.dot(xb, wu_ref[:, off:off + fc], preferred_element_type=_F32)
        h_ref[:, off:off + fc] = (gate * jax.nn.sigmoid(gate) * up).astype(_BF16)
        off += fc
    y = jnp.dot(h_ref[...], wd_ref[...], preferred_element_type=_F32)
    o_ref[...] = _layer_norm(alpha * x + 0.5 * y, g_ref[...], b_ref[...])


def _ffn_ln(x, wg, wu, wd, g, b, layer, alpha):
    seq, d = x.shape
    f = wg.shape[-1]
    tm = FFN_TILE_M
    assert seq % tm == 0 and f % 256 == 0
    f_chunks = tuple([512] * (f // 512) + ([f % 512] if f % 512 else []))
    resident = dict(pipeline_mode=pl.Buffered(1))
    wspec_in = pl.BlockSpec((None, d, f), lambda i: (layer, 0, 0), **resident)
    wspec_out = pl.BlockSpec((None, f, d), lambda i: (layer, 0, 0), **resident)
    vspec = pl.BlockSpec((None, 1, d), lambda i: (layer, 0, 0))
    row = pl.BlockSpec((tm, d), lambda i: (i, 0))
    return pl.pallas_call(
        functools.partial(_ffn_ln_kernel, alpha=alpha, f_chunks=f_chunks),
        grid=(seq // tm,),
        in_specs=[row, wspec_in, wspec_in, wspec_out, vspec, vspec],
        out_specs=row,
        out_shape=jax.ShapeDtypeStruct((seq, d), _F32),
        scratch_shapes=[pltpu.VMEM((tm, f), _BF16)],
        compiler_params=pltpu.CompilerParams(
            dimension_semantics=("arbitrary",), vmem_limit_bytes=VMEM_LIMIT_BYTES),
        name="ffn_ln",
    )(x, wg, wu, wd, g, b)


def _mixer_ln_kernel(xp_ref, x_ref, xn_ref, win_ref, wpool_ref, pscale_ref, sw_ref, cw_ref,
                     cb_ref, cg_ref, cbeta_ref, wout_ref, g_ref, b_ref, o_ref,
                     u_ref, cv_ref, a_ref, acc_ref, y_ref, *, alpha, seq, widths):
    pool_w, sconv_w, cconv_w = widths
    tm = x_ref.shape[0]
    i = pl.program_id(0)
    last = pl.num_programs(0) - 1

    x = x_ref[...]
    xp = jnp.where(i > 0, xp_ref[...], 0.0)
    xn = jnp.where(i < last, xn_ref[...], 0.0)
    xe = jnp.concatenate([xp, x, xn], axis=0).astype(_BF16)
    proj = jnp.dot(xe, win_ref[...], preferred_element_type=_F32)

    c0 = pool_w
    c1 = c0 + sconv_w
    c2 = c1 + sconv_w
    c3 = c2 + sconv_w
    c4 = c3 + cconv_w
    u_ref[...] = proj[:, :c0]
    gate_b = proj[HALO:HALO + tm, c0:c1]
    cv_ref[...] = proj[:, c1:c2] * proj[:, c2:c3]
    a_ref[0] = proj[:, c3:c4] * jax.nn.sigmoid(proj[:, c4:])

    def shifted_u(j):
        return u_ref[pl.ds(HALO + j, tm), :]

    sums = {}
    run = shifted_u(-1) + shifted_u(0)
    sums[2] = run
    lo, hi = -1, 0
    for w in POOL_WINDOWS[1:]:
        for j in list(range(-(w // 2), lo)) + list(range(hi + 1, w // 2)):
            run = run + shifted_u(j)
        lo, hi = -(w // 2), w // 2 - 1
        sums[w] = run
    group = pool_w // len(POOL_WINDOWS)
    lane_group = lax.broadcasted_iota(jnp.int32, (tm, pool_w), 1) // group
    pos = lax.broadcasted_iota(jnp.int32, (tm, pool_w), 0) + i * tm
    total = sums[POOL_WINDOWS[0]]
    window = jnp.full((tm, pool_w), POOL_WINDOWS[0], jnp.int32)
    for gi, w in enumerate(POOL_WINDOWS[1:], start=1):
        total = jnp.where(lane_group == gi, sums[w], total)
        window = jnp.where(lane_group == gi, w, window)
    start = pos - window // 2
    count = jnp.clip(start + window, 0, seq) - jnp.clip(start, 0, seq)
    pooled = total / count.astype(_F32) - shifted_u(0)
    y_a = jnp.dot(pooled.astype(_BF16), wpool_ref[...], preferred_element_type=_F32) * pscale_ref[...]
    y_ref[:, :c0] = y_a.astype(_BF16)

    conv_b = (cv_ref[pl.ds(HALO - 1, tm), :] * sw_ref[0:1, :]
              + cv_ref[pl.ds(HALO, tm), :] * sw_ref[1:2, :]
              + cv_ref[pl.ds(HALO + 1, tm), :] * sw_ref[2:3, :])
    y_ref[:, c0:c1] = (gate_b * conv_b).astype(_BF16)

    taps = cw_ref.shape[0]
    rows = MIX_CONV_ROWS
    ext = a_ref.shape[1]
    for s in range(1, SUBLANES):
        a_ref[s, pl.ds(0, ext - SUBLANES), :] = a_ref[0, pl.ds(s, ext - SUBLANES), :]

    def conv_rows(r, carry):
        base = pl.multiple_of(r * rows, rows)
        acc = jnp.zeros((rows, cconv_w), _F32)
        for k in range(taps):
            first = HALO - taps // 2 + k
            aligned = first // SUBLANES * SUBLANES
            acc = acc + a_ref[first % SUBLANES, pl.ds(base + aligned, rows), :] * cw_ref[k:k + 1, :]
        acc_ref[pl.ds(base, rows), :] = acc
        return carry

    lax.fori_loop(0, tm // rows, conv_rows, 0)
    conv_c = _layer_norm(acc_ref[...] + cb_ref[...], cg_ref[...], cbeta_ref[...])
    y_ref[:, c1:] = (conv_c * jax.nn.sigmoid(conv_c)).astype(_BF16)

    mixed = jnp.dot(y_ref[...], wout_ref[...], preferred_element_type=_F32)
    o_ref[...] = _layer_norm(alpha * x + mixed, g_ref[...], b_ref[...])


def _mixer_ln(x, w_in, w_pool, pool_scale, sconv_w, cconv_w, cconv_b, cnorm_g, cnorm_b, w_out,
              g, b, layer, alpha):
    seq, d = x.shape
    in_cols = w_in.shape[-1]
    pool_w = w_pool.shape[-1]
    sconv_width = sconv_w.shape[-1]
    cconv_width = cconv_w.shape[-1]
    mix_w = w_out.shape[-2]
    tm = MIX_TILE_M
    assert seq % tm == 0 and tm % HALO == 0 and tm % MIX_CONV_ROWS == 0
    assert cconv_w.shape[-2] // 2 < HALO and max(POOL_WINDOWS) // 2 <= HALO
    assert pool_w + 3 * sconv_width + 2 * cconv_width == in_cols
    assert pool_w + sconv_width + cconv_width == mix_w
    halo_blocks = tm // HALO
    n_halo = seq // HALO

    def layer_spec(*shape):
        return pl.BlockSpec((None,) + shape, lambda i: (layer,) + (0,) * len(shape))

    in_specs = [
        pl.BlockSpec((HALO, d), lambda i: (jnp.maximum(i * halo_blocks - 1, 0), 0)),
        pl.BlockSpec((tm, d), lambda i: (i, 0)),
        pl.BlockSpec((HALO, d), lambda i: (jnp.minimum((i + 1) * halo_blocks, n_halo - 1), 0)),
        layer_spec(d, in_cols),
        layer_spec(pool_w, pool_w),
        layer_spec(1, pool_w),
        layer_spec(sconv_w.shape[-2], sconv_width),
        layer_spec(cconv_w.shape[-2], cconv_width),
        layer_spec(1, cconv_width),
        layer_spec(1, cconv_width),
        layer_spec(1, cconv_width),
        layer_spec(mix_w, d),
        layer_spec(1, d),
        layer_spec(1, d),
    ]
    ext = tm + 2 * HALO
    return pl.pallas_call(
        functools.partial(_mixer_ln_kernel, alpha=alpha, seq=seq,
                          widths=(pool_w, sconv_width, cconv_width)),
        grid=(seq // tm,),
        in_specs=in_specs,
        out_specs=pl.BlockSpec((tm, d), lambda i: (i, 0)),
        out_shape=jax.ShapeDtypeStruct((seq, d), _F32),
        scratch_shapes=[
            pltpu.VMEM((ext, pool_w), _F32),
            pltpu.VMEM((ext, sconv_width), _F32),
            pltpu.VMEM((SUBLANES, ext, cconv_width), _F32),
            pltpu.VMEM((tm, cconv_width), _F32),
            pltpu.VMEM((tm, mix_w), _BF16),
        ],
        compiler_params=pltpu.CompilerParams(
            dimension_semantics=("arbitrary",), vmem_limit_bytes=VMEM_LIMIT_BYTES),
        name="mixer_ln",
    )(x, x, x, w_in, w_pool, pool_scale, sconv_w, cconv_w, cconv_b, cnorm_g, cnorm_b, w_out, g, b)


def _block_diag(w):
    depth, groups, c, _ = w.shape
    eye = jnp.eye(groups, dtype=w.dtype)
    return jnp.einsum("lgcd,gh->lgchd", w, eye).reshape(depth, groups * c, groups * c)


def kernel(x, ln1_g, ln1_b, ffn1_w_gate, ffn1_w_up, ffn1_w_down, mix_w_in, pool_w, pool_scale,
           sconv_w, cconv_w, cconv_b, cnorm_g, cnorm_b, mix_w_out, ln2_g, ln2_b, ffn2_w_gate,
           ffn2_w_up, ffn2_w_down, ln3_g, ln3_b):
    bsz, seq, d = x.shape
    depth = ln1_g.shape[0]
    alpha = (2.0 * depth) ** 0.25

    def bf(w):
        return w.astype(_BF16)

    def vec(v):
        return v[:, None, :]

    f1 = (bf(ffn1_w_gate), bf(ffn1_w_up), bf(ffn1_w_down), vec(ln1_g), vec(ln1_b))
    f2 = (bf(ffn2_w_gate), bf(ffn2_w_up), bf(ffn2_w_down), vec(ln3_g), vec(ln3_b))
    mx = (bf(mix_w_in), bf(_block_diag(pool_w)), vec(pool_scale), sconv_w, cconv_w,
          vec(cconv_b), vec(cnorm_g), vec(cnorm_b), bf(mix_w_out), vec(ln2_g), vec(ln2_b))

    outs = []
    for bi in range(bsz):
        h = x[bi]
        for l in range(depth):
            h = _ffn_ln(h, *f1, l, alpha)
            h = _mixer_ln(h, *mx, l, alpha)
            h = _ffn_ln(h, *f2, l, alpha)
        outs.append(h)
    return jnp.stack(outs, axis=0) if bsz > 1 else outs[0][None]
```

```python
import functools

import jax
import jax.numpy as jnp
from jax import lax
from jax.experimental import pallas as pl
from jax.experimental.pallas import tpu as pltpu

LN_EPS = 1e-5
POOL_WINDOWS = (2, 4, 8, 16)
SUBLANES = 8
LANES = 128
HALO = 16
FFN_TILE_M = 512
MIX_TILE_M = 512
MIX_CONV_ROWS = 32
MXU_N_CHUNK = 512
VMEM_LIMIT_BYTES = 56 * 1024 * 1024

_F32 = jnp.float32
_BF16 = jnp.bfloat16


def _layer_norm(z, g, b):
    mu = jnp.mean(z, axis=-1, keepdims=True)
    zc = z - mu
    var = jnp.mean(zc * zc, axis=-1, keepdims=True)
    return zc * lax.rsqrt(var + LN_EPS) * g + b


def _ffn_ln_kernel(x_ref, wg_ref, wu_ref, wd_ref, g_ref, b_ref, o_ref, h_ref, *, alpha, f_chunks):
    x = x_ref[...]
    xb = x.astype(_BF16)
    off = 0
    for fc in f_chunks:
        gate = jnp.dot(xb, wg_ref[:, off:off + fc], preferred_element_type=_F32)
        up = jnp.dot(xb, wu_ref[:, off:off + fc], preferred_element_type=_F32)
        h_ref[:, off:off + fc] = (gate * jax.nn.sigmoid(gate) * up).astype(_BF16)
        off += fc
    y = jnp.dot(h_ref[...], wd_ref[...], preferred_element_type=_F32)
    o_ref[...] = _layer_norm(alpha * x + 0.5 * y, g_ref[...], b_ref[...])


def _ffn_ln(x, wg, wu, wd, g, b, layer, alpha):
    seq, d = x.shape
    f = wg.shape[-1]
    tm = FFN_TILE_M
    assert seq % tm == 0 and f % 256 == 0
    f_chunks = tuple([512] * (f // 512) + ([f % 512] if f % 512 else []))
    resident = dict(pipeline_mode=pl.Buffered(1))
    wspec_in = pl.BlockSpec((None, d, f), lambda i: (layer, 0, 0), **resident)
    wspec_out = pl.BlockSpec((None, f, d), lambda i: (layer, 0, 0), **resident)
    vspec = pl.BlockSpec((None, 1, d), lambda i: (layer, 0, 0))
    row = pl.BlockSpec((tm, d), lambda i: (i, 0))
    return pl.pallas_call(
        functools.partial(_ffn_ln_kernel, alpha=alpha, f_chunks=f_chunks),
        grid=(seq // tm,),
        in_specs=[row, wspec_in, wspec_in, wspec_out, vspec, vspec],
        out_specs=row,
        out_shape=jax.ShapeDtypeStruct((seq, d), _F32),
        scratch_shapes=[pltpu.VMEM((tm, f), _BF16)],
        compiler_params=pltpu.CompilerParams(
            dimension_semantics=("arbitrary",), vmem_limit_bytes=VMEM_LIMIT_BYTES),
        name="ffn_ln",
    )(x, wg, wu, wd, g, b)


def _column_chunks(ranges):
    out = []
    for lo, hi in ranges:
        for n0 in range(lo, hi, MXU_N_CHUNK):
            out.append((n0, min(n0 + MXU_N_CHUNK, hi)))
    return out


def _mixer_ln_kernel(xp_ref, x_ref, xn_ref, win_ref, wpool_ref, pscale_ref, sw_ref,
                     cw_ref, cb_ref, cg_ref, cbeta_ref, wout_ref, g_ref, b_ref, o_ref,
                     xe_ref, proj_ref, u_ref, cv_ref, a_ref, conv_ref, pooled_ref, y_ref,
                     *, alpha, seq, widths):
    pool_w, sconv_w, cconv_w = widths
    tm = x_ref.shape[0]
    tile = pl.program_id(0)
    last = pl.num_programs(0) - 1
    c0 = pool_w
    c1 = c0 + sconv_w
    c2 = c1 + sconv_w
    c3 = c2 + sconv_w
    c4 = c3 + cconv_w
    c5 = c4 + cconv_w

    def lanes(c, base=0):
        return slice(base + c * LANES, base + (c + 1) * LANES)

    x = x_ref[...]
    xe_ref[pl.ds(0, HALO), :] = jnp.where(tile > 0, xp_ref[...], 0.0).astype(_BF16)
    xe_ref[pl.ds(HALO, tm), :] = x.astype(_BF16)
    xe_ref[pl.ds(HALO + tm, HALO), :] = jnp.where(tile < last, xn_ref[...], 0.0).astype(_BF16)
    for n0, n1 in _column_chunks([(c3, c5), (c1, c3), (0, c1)]):
        proj_ref[:, n0:n1] = jnp.dot(xe_ref[...], win_ref[:, n0:n1], preferred_element_type=_F32)

    for c in range(cconv_w // LANES):
        a_ref[c] = proj_ref[:, lanes(c, c3)] * jax.nn.sigmoid(proj_ref[:, lanes(c, c4)])
    for c in range(sconv_w // LANES):
        cv_ref[c] = proj_ref[:, lanes(c, c1)] * proj_ref[:, lanes(c, c2)]
    for c in range(pool_w // LANES):
        u_ref[c] = proj_ref[:, lanes(c)]

    taps = cw_ref.shape[1]
    for c in range(cconv_w // LANES):
        for r in range(0, tm, MIX_CONV_ROWS):
            accs = [None] * (MIX_CONV_ROWS // SUBLANES)
            for k in range(taps):
                w_k = jnp.broadcast_to(cw_ref[c, k:k + 1, :], (SUBLANES, LANES))
                for j in range(len(accs)):
                    first = r + j * SUBLANES + HALO - taps // 2 + k
                    term = a_ref[c, pl.ds(first, SUBLANES), :] * w_k
                    accs[j] = term if accs[j] is None else accs[j] + term
            for j, acc in enumerate(accs):
                conv_ref[pl.ds(r + j * SUBLANES, SUBLANES), lanes(c)] = acc
    conv_c = _layer_norm(conv_ref[...] + cb_ref[...], cg_ref[...], cbeta_ref[...])
    y_ref[:, c1:] = (conv_c * jax.nn.sigmoid(conv_c)).astype(_BF16)

    for c in range(sconv_w // LANES):
        conv_b = cv_ref[c, pl.ds(HALO - 1, tm), :] * sw_ref[c, 0:1, :]
        for k in range(1, sw_ref.shape[1]):
            conv_b = conv_b + cv_ref[c, pl.ds(HALO - 1 + k, tm), :] * sw_ref[c, k:k + 1, :]
        y_ref[:, lanes(c, c0)] = (proj_ref[pl.ds(HALO, tm), lanes(c, c0)] * conv_b).astype(_BF16)

    group = pool_w // len(POOL_WINDOWS)
    groups_per_slab = LANES // group
    lane_group = lax.broadcasted_iota(jnp.int32, (tm, LANES), 1) // group
    pos = lax.broadcasted_iota(jnp.int32, (tm, LANES), 0) + tile * tm
    for c in range(pool_w // LANES):
        def shifted_u(j, c=c):
            return u_ref[c, pl.ds(HALO + j, tm), :]

        run = total = window = None
        lo, hi = 0, -1
        for gi in range(groups_per_slab):
            w = POOL_WINDOWS[c * groups_per_slab + gi]
            new_lo, new_hi = -(w // 2), w - w // 2 - 1
            for j in list(range(new_lo, lo)) + list(range(hi + 1, new_hi + 1)):
                run = shifted_u(j) if run is None else run + shifted_u(j)
            lo, hi = new_lo, new_hi
            total = run if total is None else jnp.where(lane_group == gi, run, total)
            window = (jnp.full((tm, LANES), w, jnp.int32) if window is None
                      else jnp.where(lane_group == gi, w, window))
        start = pos - window // 2
        count = jnp.clip(start + window, 0, seq) - jnp.clip(start, 0, seq)
        pooled_ref[:, lanes(c)] = (total / count.astype(_F32) - shifted_u(0)).astype(_BF16)
    y_a = jnp.dot(pooled_ref[...], wpool_ref[...], preferred_element_type=_F32) * pscale_ref[...]
    y_ref[:, :c0] = y_a.astype(_BF16)

    mixed = jnp.dot(y_ref[...], wout_ref[...], preferred_element_type=_F32)
    o_ref[...] = _layer_norm(alpha * x + mixed, g_ref[...], b_ref[...])


def _lane_slabs(w):
    depth, k, c = w.shape
    return w.reshape(depth, k, c // LANES, LANES).transpose(0, 2, 1, 3)


def _mixer_ln(x, w_in, w_pool, pool_scale, sconv_w, cconv_w, cconv_b, cnorm_g, cnorm_b, w_out,
              g, b, layer, alpha):
    seq, d = x.shape
    in_cols = w_in.shape[-1]
    pool_w = w_pool.shape[-1]
    sconv_width = sconv_w.shape[-1]
    cconv_width = cconv_w.shape[-1]
    mix_w = w_out.shape[-2]
    tm = MIX_TILE_M
    assert seq % tm == 0 and tm % HALO == 0 and tm % MIX_CONV_ROWS == 0
    assert cconv_w.shape[-2] // 2 < HALO and max(POOL_WINDOWS) // 2 <= HALO
    assert pool_w + 3 * sconv_width + 2 * cconv_width == in_cols
    assert pool_w + sconv_width + cconv_width == mix_w
    assert pool_w % LANES == 0 and sconv_width % LANES == 0 and cconv_width % LANES == 0
    assert LANES % (pool_w // len(POOL_WINDOWS)) == 0 and list(POOL_WINDOWS) == sorted(POOL_WINDOWS)
    halo_blocks = tm // HALO
    n_halo = seq // HALO

    def layer_spec(*shape):
        return pl.BlockSpec((None,) + shape, lambda i: (layer,) + (0,) * len(shape))

    sconv_slabs = _lane_slabs(sconv_w)
    cconv_slabs = _lane_slabs(cconv_w)
    in_specs = [
        pl.BlockSpec((HALO, d), lambda i: (jnp.maximum(i * halo_blocks - 1, 0), 0)),
        pl.BlockSpec((tm, d), lambda i: (i, 0)),
        pl.BlockSpec((HALO, d), lambda i: (jnp.minimum((i + 1) * halo_blocks, n_halo - 1), 0)),
        layer_spec(d, in_cols),
        layer_spec(pool_w, pool_w),
        layer_spec(1, pool_w),
        layer_spec(*sconv_slabs.shape[1:]),
        layer_spec(*cconv_slabs.shape[1:]),
        layer_spec(1, cconv_width),
        layer_spec(1, cconv_width),
        layer_spec(1, cconv_width),
        layer_spec(mix_w, d),
        layer_spec(1, d),
        layer_spec(1, d),
    ]
    ext = tm + 2 * HALO
    return pl.pallas_call(
        functools.partial(_mixer_ln_kernel, alpha=alpha, seq=seq,
                          widths=(pool_w, sconv_width, cconv_width)),
        grid=(seq // tm,),
        in_specs=in_specs,
        out_specs=pl.BlockSpec((tm, d), lambda i: (i, 0)),
        out_shape=jax.ShapeDtypeStruct((seq, d), _F32),
        scratch_shapes=[
            pltpu.VMEM((ext, d), _BF16),
            pltpu.VMEM((ext, in_cols), _F32),
            pltpu.VMEM((pool_w // LANES, ext, LANES), _F32),
            pltpu.VMEM((sconv_width // LANES, ext, LANES), _F32),
            pltpu.VMEM((cconv_width // LANES, ext, LANES), _F32),
            pltpu.VMEM((tm, cconv_width), _F32),
            pltpu.VMEM((tm, pool_w), _BF16),
            pltpu.VMEM((tm, mix_w), _BF16),
        ],
        compiler_params=pltpu.CompilerParams(
            dimension_semantics=("arbitrary",), vmem_limit_bytes=VMEM_LIMIT_BYTES),
        name="mixer_ln",
    )(x, x, x, w_in, w_pool, pool_scale, sconv_slabs, cconv_slabs, cconv_b, cnorm_g, cnorm_b,
      w_out, g, b)


def _block_diag(w):
    depth, groups, c, _ = w.shape
    eye = jnp.eye(groups, dtype=w.dtype)
    return jnp.einsum("lgcd,gh->lgchd", w, eye).reshape(depth, groups * c, groups * c)


def kernel(x, ln1_g, ln1_b, ffn1_w_gate, ffn1_w_up, ffn1_w_down, mix_w_in, pool_w, pool_scale,
           sconv_w, cconv_w, cconv_b, cnorm_g, cnorm_b, mix_w_out, ln2_g, ln2_b, ffn2_w_gate,
           ffn2_w_up, ffn2_w_down, ln3_g, ln3_b):
    bsz, seq, d = x.shape
    depth = ln1_g.shape[0]
    alpha = (2.0 * depth) ** 0.25

    def bf(w):
        return w.astype(_BF16)

    def vec(v):
        return v[:, None, :]

    f1 = (bf(ffn1_w_gate), bf(ffn1_w_up), bf(ffn1_w_down), vec(ln1_g), vec(ln1_b))
    f2 = (bf(ffn2_w_gate), bf(ffn2_w_up), bf(ffn2_w_down), vec(ln3_g), vec(ln3_b))
    mx = (bf(mix_w_in), bf(_block_diag(pool_w)), vec(pool_scale), sconv_w, cconv_w,
          vec(cconv_b), vec(cnorm_g), vec(cnorm_b), bf(mix_w_out), vec(ln2_g), vec(ln2_b))

    outs = []
    for bi in range(bsz):
        h = x[bi]
        for l in range(depth):
            h = _ffn_ln(h, *f1, l, alpha)
            h = _mixer_ln(h, *mx, l, alpha)
            h = _ffn_ln(h, *f2, l, alpha)
        outs.append(h)
    return jnp.stack(outs, axis=0) if bsz > 1 else outs[0][None]
```

```python
import functools

import jax
import jax.numpy as jnp
from jax import lax
from jax.experimental import pallas as pl
from jax.experimental.pallas import tpu as pltpu

LN_EPS = 1e-5
POOL_WINDOWS = (2, 4, 8, 16)
SUBLANES = 8
LANES = 128
HALO = 16
FFN_TILE_M = 1024
FFN_DOWN_ROWS = 512
MIX_TILE_M = 1024
MIX_CONV_ROWS = 32
MXU_N_CHUNK = 512
VMEM_LIMIT_BYTES = 56 * 1024 * 1024

_F32 = jnp.float32
_BF16 = jnp.bfloat16


def _layer_norm(z, g, b):
    mu = jnp.mean(z, axis=-1, keepdims=True)
    zc = z - mu
    var = jnp.mean(zc * zc, axis=-1, keepdims=True)
    return zc * lax.rsqrt(var + LN_EPS) * g + b


def _ffn_ln_kernel(x_ref, wg_ref, wu_ref, wd_ref, g_ref, b_ref, o_ref, h_ref, *, alpha, f_chunks):
    x = x_ref[...]
    xb = x.astype(_BF16)
    off = 0
    for fc in f_chunks:
        gate = jnp.dot(xb, wg_ref[:, off:off + fc], preferred_element_type=_F32)
        up = jnp.dot(xb, wu_ref[:, off:off + fc], preferred_element_type=_F32)
        h_ref[:, off:off + fc] = (gate * jax.nn.sigmoid(gate) * up).astype(_BF16)
        off += fc
    for r0 in range(0, x.shape[0], FFN_DOWN_ROWS):
        rows = pl.ds(r0, FFN_DOWN_ROWS)
        y = jnp.dot(h_ref[rows, :], wd_ref[...], preferred_element_type=_F32)
        o_ref[rows, :] = _layer_norm(alpha * x_ref[rows, :] + 0.5 * y, g_ref[...], b_ref[...])


def _ffn_ln(x, wg, wu, wd, g, b, layer, alpha):
    seq, d = x.shape
    f = wg.shape[-1]
    tm = FFN_TILE_M
    assert seq % tm == 0 and f % 256 == 0 and tm % FFN_DOWN_ROWS == 0
    f_chunks = tuple([512] * (f // 512) + ([f % 512] if f % 512 else []))
    resident = dict(pipeline_mode=pl.Buffered(1))
    wspec_in = pl.BlockSpec((None, d, f), lambda i: (layer, 0, 0), **resident)
    wspec_out = pl.BlockSpec((None, f, d), lambda i: (layer, 0, 0), **resident)
    vspec = pl.BlockSpec((None, 1, d), lambda i: (layer, 0, 0))
    row = pl.BlockSpec((tm, d), lambda i: (i, 0))
    return pl.pallas_call(
        functools.partial(_ffn_ln_kernel, alpha=alpha, f_chunks=f_chunks),
        grid=(seq // tm,),
        in_specs=[row, wspec_in, wspec_in, wspec_out, vspec, vspec],
        out_specs=row,
        out_shape=jax.ShapeDtypeStruct((seq, d), _F32),
        scratch_shapes=[pltpu.VMEM((tm, f), _BF16)],
        compiler_params=pltpu.CompilerParams(
            dimension_semantics=("arbitrary",), vmem_limit_bytes=VMEM_LIMIT_BYTES),
        name="ffn_ln",
    )(x, wg, wu, wd, g, b)


def _column_chunks(ranges):
    out = []
    for lo, hi in ranges:
        for n0 in range(lo, hi, MXU_N_CHUNK):
            out.append((n0, min(n0 + MXU_N_CHUNK, hi)))
    return out


def _mixer_ln_kernel(xp_ref, x_ref, xn_ref, win_ref, wpool_ref, pscale_ref, sw_ref,
                     cw_ref, cb_ref, cg_ref, cbeta_ref, wout_ref, g_ref, b_ref, o_ref,
                     xe_ref, proj_ref, u_ref, cv_ref, a_ref, conv_ref, pooled_ref, y_ref,
                     *, alpha, seq, widths):
    pool_w, sconv_w, cconv_w = widths
    tm = x_ref.shape[0]
    tile = pl.program_id(0)
    last = pl.num_programs(0) - 1
    c0 = pool_w
    c1 = c0 + sconv_w
    c2 = c1 + sconv_w
    c3 = c2 + sconv_w
    c4 = c3 + cconv_w
    c5 = c4 + cconv_w

    def lanes(c, base=0):
        return slice(base + c * LANES, base + (c + 1) * LANES)

    x = x_ref[...]
    xe_ref[pl.ds(0, HALO), :] = jnp.where(tile > 0, xp_ref[...], 0.0).astype(_BF16)
    xe_ref[pl.ds(HALO, tm), :] = x.astype(_BF16)
    xe_ref[pl.ds(HALO + tm, HALO), :] = jnp.where(tile < last, xn_ref[...], 0.0).astype(_BF16)
    for n0, n1 in _column_chunks([(c3, c5), (c1, c3), (0, c1)]):
        proj_ref[:, n0:n1] = jnp.dot(xe_ref[...], win_ref[:, n0:n1], preferred_element_type=_F32)

    for c in range(cconv_w // LANES):
        a_ref[c] = proj_ref[:, lanes(c, c3)] * jax.nn.sigmoid(proj_ref[:, lanes(c, c4)])
    for c in range(sconv_w // LANES):
        cv_ref[c] = proj_ref[:, lanes(c, c1)] * proj_ref[:, lanes(c, c2)]
    for c in range(pool_w // LANES):
        u_ref[c] = proj_ref[:, lanes(c)]

    taps = cw_ref.shape[1]
    for c in range(cconv_w // LANES):
        for r in range(0, tm, MIX_CONV_ROWS):
            accs = [None] * (MIX_CONV_ROWS // SUBLANES)
            for k in range(taps):
                w_k = jnp.broadcast_to(cw_ref[c, k:k + 1, :], (SUBLANES, LANES))
                for j in range(len(accs)):
                    first = r + j * SUBLANES + HALO - taps // 2 + k
                    term = a_ref[c, pl.ds(first, SUBLANES), :] * w_k
                    accs[j] = term if accs[j] is None else accs[j] + term
            for j, acc in enumerate(accs):
                conv_ref[pl.ds(r + j * SUBLANES, SUBLANES), lanes(c)] = acc
    conv_c = _layer_norm(conv_ref[...] + cb_ref[...], cg_ref[...], cbeta_ref[...])
    y_ref[:, c1:] = (conv_c * jax.nn.sigmoid(conv_c)).astype(_BF16)

    for c in range(sconv_w // LANES):
        conv_b = cv_ref[c, pl.ds(HALO - 1, tm), :] * sw_ref[c, 0:1, :]
        for k in range(1, sw_ref.shape[1]):
            conv_b = conv_b + cv_ref[c, pl.ds(HALO - 1 + k, tm), :] * sw_ref[c, k:k + 1, :]
        y_ref[:, lanes(c, c0)] = (proj_ref[pl.ds(HALO, tm), lanes(c, c0)] * conv_b).astype(_BF16)

    group = pool_w // len(POOL_WINDOWS)
    groups_per_slab = LANES // group
    lane_group = lax.broadcasted_iota(jnp.int32, (tm, LANES), 1) // group
    pos = lax.broadcasted_iota(jnp.int32, (tm, LANES), 0) + tile * tm
    for c in range(pool_w // LANES):
        def shifted_u(j, c=c):
            return u_ref[c, pl.ds(HALO + j, tm), :]

        run = total = window = None
        lo, hi = 0, -1
        for gi in range(groups_per_slab):
            w = POOL_WINDOWS[c * groups_per_slab + gi]
            new_lo, new_hi = -(w // 2), w - w // 2 - 1
            for j in list(range(new_lo, lo)) + list(range(hi + 1, new_hi + 1)):
                run = shifted_u(j) if run is None else run + shifted_u(j)
            lo, hi = new_lo, new_hi
            total = run if total is None else jnp.where(lane_group == gi, run, total)
            window = (jnp.full((tm, LANES), w, jnp.int32) if window is None
                      else jnp.where(lane_group == gi, w, window))
        start = pos - window // 2
        count = jnp.clip(start + window, 0, seq) - jnp.clip(start, 0, seq)
        pooled_ref[:, lanes(c)] = (total / count.astype(_F32) - shifted_u(0)).astype(_BF16)
    y_a = jnp.dot(pooled_ref[...], wpool_ref[...], preferred_element_type=_F32) * pscale_ref[...]
    y_ref[:, :c0] = y_a.astype(_BF16)

    mixed = jnp.dot(y_ref[...], wout_ref[...], preferred_element_type=_F32)
    o_ref[...] = _layer_norm(alpha * x + mixed, g_ref[...], b_ref[...])


def _lane_slabs(w):
    depth, k, c = w.shape
    return w.reshape(depth, k, c // LANES, LANES).transpose(0, 2, 1, 3)


def _mixer_ln(x, w_in, w_pool, pool_scale, sconv_w, cconv_w, cconv_b, cnorm_g, cnorm_b, w_out,
              g, b, layer, alpha):
    seq, d = x.shape
    in_cols = w_in.shape[-1]
    pool_w = w_pool.shape[-1]
    sconv_width = sconv_w.shape[-1]
    cconv_width = cconv_w.shape[-1]
    mix_w = w_out.shape[-2]
    tm = MIX_TILE_M
    assert seq % tm == 0 and tm % HALO == 0 and tm % MIX_CONV_ROWS == 0
    assert cconv_w.shape[-2] // 2 < HALO and max(POOL_WINDOWS) // 2 <= HALO
    assert pool_w + 3 * sconv_width + 2 * cconv_width == in_cols
    assert pool_w + sconv_width + cconv_width == mix_w
    assert pool_w % LANES == 0 and sconv_width % LANES == 0 and cconv_width % LANES == 0
    assert LANES % (pool_w // len(POOL_WINDOWS)) == 0 and list(POOL_WINDOWS) == sorted(POOL_WINDOWS)
    halo_blocks = tm // HALO
    n_halo = seq // HALO

    def layer_spec(*shape):
        return pl.BlockSpec((None,) + shape, lambda i: (layer,) + (0,) * len(shape))

    sconv_slabs = _lane_slabs(sconv_w)
    cconv_slabs = _lane_slabs(cconv_w)
    in_specs = [
        pl.BlockSpec((HALO, d), lambda i: (jnp.maximum(i * halo_blocks - 1, 0), 0)),
        pl.BlockSpec((tm, d), lambda i: (i, 0)),
        pl.BlockSpec((HALO, d), lambda i: (jnp.minimum((i + 1) * halo_blocks, n_halo - 1), 0)),
        layer_spec(d, in_cols),
        layer_spec(pool_w, pool_w),
        layer_spec(1, pool_w),
        layer_spec(*sconv_slabs.shape[1:]),
        layer_spec(*cconv_slabs.shape[1:]),
        layer_spec(1, cconv_width),
        layer_spec(1, cconv_width),
        layer_spec(1, cconv_width),
        layer_spec(mix_w, d),
        layer_spec(1, d),
        layer_spec(1, d),
    ]
    ext = tm + 2 * HALO
    return pl.pallas_call(
        functools.partial(_mixer_ln_kernel, alpha=alpha, seq=seq,
                          widths=(pool_w, sconv_width, cconv_width)),
        grid=(seq // tm,),
        in_specs=in_specs,
        out_specs=pl.BlockSpec((tm, d), lambda i: (i, 0)),
        out_shape=jax.ShapeDtypeStruct((seq, d), _F32),
        scratch_shapes=[
            pltpu.VMEM((ext, d), _BF16),
            pltpu.VMEM((ext, in_cols), _F32),
            pltpu.VMEM((pool_w // LANES, ext, LANES), _F32),
            pltpu.VMEM((sconv_width // LANES, ext, LANES), _F32),
            pltpu.VMEM((cconv_width // LANES, ext, LANES), _F32),
            pltpu.VMEM((tm, cconv_width), _F32),
            pltpu.VMEM((tm, pool_w), _BF16),
            pltpu.VMEM((tm, mix_w), _BF16),
        ],
        compiler_params=pltpu.CompilerParams(
            dimension_semantics=("arbitrary",), vmem_limit_bytes=VMEM_LIMIT_BYTES),
        name="mixer_ln",
    )(x, x, x, w_in, w_pool, pool_scale, sconv_slabs, cconv_slabs, cconv_b, cnorm_g, cnorm_b,
      w_out, g, b)


def _block_diag(w):
    depth, groups, c, _ = w.shape
    eye = jnp.eye(groups, dtype=w.dtype)
    return jnp.einsum("lgcd,gh->lgchd", w, eye).reshape(depth, groups * c, groups * c)


def kernel(x, ln1_g, ln1_b, ffn1_w_gate, ffn1_w_up, ffn1_w_down, mix_w_in, pool_w, pool_scale,
           sconv_w, cconv_w, cconv_b, cnorm_g, cnorm_b, mix_w_out, ln2_g, ln2_b, ffn2_w_gate,
           ffn2_w_up, ffn2_w_down, ln3_g, ln3_b):
    bsz, seq, d = x.shape
    depth = ln1_g.shape[0]
    alpha = (2.0 * depth) ** 0.25

    def bf(w):
        return w.astype(_BF16)

    def vec(v):
        return v[:, None, :]

    f1 = (bf(ffn1_w_gate), bf(ffn1_w_up), bf(ffn1_w_down), vec(ln1_g), vec(ln1_b))
    f2 = (bf(ffn2_w_gate), bf(ffn2_w_up), bf(ffn2_w_down), vec(ln3_g), vec(ln3_b))
    mx = (bf(mix_w_in), bf(_block_diag(pool_w)), vec(pool_scale), sconv_w, cconv_w,
          vec(cconv_b), vec(cnorm_g), vec(cnorm_b), bf(mix_w_out), vec(ln2_g), vec(ln2_b))

    outs = []
    for bi in range(bsz):
        h = x[bi]
        for l in range(depth):
            h = _ffn_ln(h, *f1, l, alpha)
            h = _mixer_ln(h, *mx, l, alpha)
            h = _ffn_ln(h, *f2, l, alpha)
        outs.append(h)
    return jnp.stack(outs, axis=0) if bsz > 1 else outs[0][None]
```
